```python
import math
import jax, jax.numpy as jnp
from jax import lax
import numpy as np

D_MODEL = 4096
BATCH = 4
SEQ = 2048
DEPTH = 1
DEC_BATCH = 128
DEC_SEQ = 8
PAST_LEN = 2048
PAGE_SIZE = 128

RET_HEADS = 8
RET_DK = D_MODEL // 16
RET_DV = D_MODEL // 16
RET_CHUNK = 128
NSA_HEADS = 16
NSA_KV_HEADS = 4
NSA_GROUP = NSA_HEADS // NSA_KV_HEADS
NSA_HD = 128
CMP_LEN = 32
CMP_STRIDE = 16
SEL_LEN = 64
SEL_TOPN = 16
WINDOW = 512
Q_BLOCK = 128
SEL_Q_BLOCK = 64
D_FF = 4 * D_MODEL
PLE_DIM = 256
EPS = 1e-6
NEG = -1e30
FORCE = 1e6

RET_QK = RET_HEADS * RET_DK
RET_VW = RET_HEADS * RET_DV
NSA_QW = NSA_HEADS * NSA_HD
NSA_KVW = NSA_KV_HEADS * NSA_HD
IN_SPLITS = [RET_QK, RET_QK, RET_VW, RET_VW, NSA_QW, 6 * NSA_KVW, NSA_HEADS * 3, D_MODEL, D_MODEL]
IN_COLS = sum(IN_SPLITS)

kernel_name = "retnet_nsa_parallel_hybrid_step"


def rms_norm(x, g):
    xf = x.astype(jnp.float32)
    y = xf * lax.rsqrt(jnp.mean(xf * xf, axis=-1, keepdims=True) + EPS)
    return (y * g.astype(jnp.float32)).astype(x.dtype)


def masked_softmax(s, valid):
    p = jax.nn.softmax(jnp.where(valid, s, NEG), axis=-1)
    return jnp.where(valid, p, 0.0)


def alibi_slopes():
    m = 2.0 ** (-8.0 * np.arange(1, NSA_HEADS + 1) / NSA_HEADS)
    return jnp.asarray(m, jnp.float32).reshape(NSA_KV_HEADS, NSA_GROUP)


def retention(q, k, v, state0):
    B, T = q.shape[:2]
    C = math.gcd(T, RET_CHUNK)
    n = T // C
    f32 = jnp.float32
    log_g = jnp.log(1.0 - 2.0 ** (-5.0 - jnp.arange(RET_HEADS, dtype=f32)))
    i = jnp.arange(C, dtype=f32)
    diff = i[:, None] - i[None, :]
    decay_mat = jnp.where(diff >= 0, jnp.exp(log_g[:, None, None] * jnp.maximum(diff, 0.0)), 0.0)
    q_decay = jnp.exp(log_g[:, None] * (i + 1.0))[None, :, :, None]
    k_decay = jnp.exp(log_g[:, None] * (C - 1.0 - i))[None, :, :, None]
    chunk_decay = jnp.exp(log_g * C)[None, :, None, None]

    def to_chunks(a):
        return a.astype(f32).reshape(B, n, C, RET_HEADS, a.shape[-1]).transpose(1, 0, 3, 2, 4)

    qc = to_chunks(q)
    kc = to_chunks(k) * (RET_DK ** -0.5)
    vc = to_chunks(v)

    def step(S, xs):
        qq, kk, vv = xs
        inner = jnp.einsum('bhid,bhjd->bhij', qq, kk) * decay_mat
        o = jnp.einsum('bhij,bhje->bhie', inner, vv) + jnp.einsum('bhid,bhde->bhie', qq, S) * q_decay
        S = S * chunk_decay + jnp.einsum('bhjd,bhje->bhde', kk * k_decay, vv)
        return S, o

    S, o = lax.scan(step, state0.astype(f32), (qc, kc, vc))
    o = o.transpose(1, 0, 3, 2, 4).reshape(B, T, RET_HEADS, RET_DV)
    return o, S


def compress(x, pos, w1, w2):
    B, T = x.shape[:2]
    R = CMP_LEN // CMP_STRIDE
    nseg = T // CMP_STRIDE
    nc = nseg - R + 1
    seg = x[:, :nseg * CMP_STRIDE].reshape(B, nseg, CMP_STRIDE, NSA_KV_HEADS, NSA_HD)
    acc = 0.0
    for r in range(R):
        sl = slice(r * CMP_STRIDE, (r + 1) * CMP_STRIDE)
        proj = jnp.einsum('bspkd,pde->bske', seg + pos[sl][None, None, :, None, :], w1[sl])
        acc = acc + proj[:, r:r + nc]
    return jnp.einsum('bcke,ef->bckf', jax.nn.gelu(acc), w2)


def overlap_matrix(nc, nb):
    start = np.arange(nc) * CMP_STRIDE
    end = start + CMP_LEN
    bs = np.arange(nb) * SEL_LEN
    be = bs + SEL_LEN
    m = (start[:, None] < be[None, :]) & (end[:, None] > bs[None, :])
    return jnp.asarray(m, jnp.float32)


def sel_one_seq(q, idx, kvb, qpos, slopes):
    Tq = q.shape[2]
    N = idx.shape[-1]
    QB = math.gcd(Tq, SEL_Q_BLOCK)
    nq = Tq // QB
    qb = q.reshape(NSA_KV_HEADS, NSA_GROUP, nq, QB, NSA_HD).transpose(2, 0, 1, 3, 4)
    ib = idx.reshape(NSA_KV_HEADS, nq, QB, N).transpose(1, 0, 2, 3)
    pb = qpos.reshape(nq, QB)
    heads = jnp.arange(NSA_KV_HEADS)[:, None, None]
    scale = NSA_HD ** -0.5

    def blk(a):
        qq, ii, pp = a
        kg = kvb[0][heads, ii]
        vg = kvb[1][heads, ii]
        kpos = ii[..., None] * SEL_LEN + jnp.arange(SEL_LEN)
        dist = (pp[None, :, None, None] - kpos).astype(jnp.float32)
        s = jnp.einsum('kgqd,kqnld->kgqnl', qq, kg).astype(jnp.float32) * scale \
            - slopes[:, :, None, None, None] * dist[:, None]
        valid = (kpos <= pp[None, :, None, None])[:, None]
        s = s.reshape(NSA_KV_HEADS, NSA_GROUP, QB, N * SEL_LEN)
        valid = valid.reshape(NSA_KV_HEADS, 1, QB, N * SEL_LEN)
        p = masked_softmax(s, valid)
        return jnp.einsum('kgqm,kqmd->kgqd', p.astype(vg.dtype), vg.reshape(NSA_KV_HEADS, QB, N * SEL_LEN, NSA_HD))

    o = lax.map(blk, (qb, ib, pb))
    return o.transpose(1, 2, 0, 3, 4).reshape(NSA_KV_HEADS, NSA_GROUP, Tq, NSA_HD)


def nsa_compressed_selected(q5, kv_all, qpos, slopes, lw):
    B, Ttot = kv_all.shape[:2]
    scale = NSA_HD ** -0.5
    ck = rms_norm(compress(kv_all[:, :, 0], lw['pos_ck'], lw['w_ck1'], lw['w_ck2']), lw['g_k_cmp'])
    cv = compress(kv_all[:, :, 1], lw['pos_cv'], lw['w_cv1'], lw['w_cv2'])
    nc = ck.shape[1]
    end_pos = jnp.arange(nc) * CMP_STRIDE + CMP_LEN - 1
    dist = (qpos[:, None] - end_pos[None, :]).astype(jnp.float32)
    s = jnp.einsum('bkgqd,bckd->bkgqc', q5, ck).astype(jnp.float32) * scale - slopes[:, :, None, None] * dist
    valid = end_pos[None, :] <= qpos[:, None]
    p = masked_softmax(s, valid)
    o_cmp = jnp.einsum('bkgqc,bckd->bkgqd', p.astype(cv.dtype), cv)
    nb = -(-Ttot // SEL_LEN)
    imp = jnp.einsum('bkgqc,cj->bkqj', p, overlap_matrix(nc, nb))
    jb = jnp.arange(nb)[None, :]
    cur = (qpos // SEL_LEN)[:, None]
    visible = jb * SEL_LEN <= qpos[:, None]
    forced = (jb == 0) | (jb == cur) | (jb == cur - 1)
    score = jnp.where(visible, imp + jnp.where(forced, FORCE, 0.0), -FORCE * FORCE)
    n_sel = min(SEL_TOPN, nb)
    _, idx = lax.top_k(score, n_sel)
    pad = nb * SEL_LEN - Ttot
    kvs = jnp.pad(kv_all[:, :, 2:4], ((0, 0), (0, pad), (0, 0), (0, 0), (0, 0)))
    kvb = kvs.reshape(B, nb, SEL_LEN, 2, NSA_KV_HEADS, NSA_HD).transpose(0, 3, 4, 1, 2, 5)
    o_sel = lax.map(lambda a: sel_one_seq(a[0], a[1], a[2], qpos, slopes), (q5, idx, kvb))
    return o_cmp, o_sel


def window_prompt(q5, kw, vw, slopes):
    B, T = kw.shape[:2]
    QB = math.gcd(T, Q_BLOCK)
    nq = T // QB
    padw = ((0, 0), (WINDOW, 0), (0, 0), (0, 0))
    kp = jnp.pad(kw, padw)
    vp = jnp.pad(vw, padw)
    idx = np.arange(nq)[:, None] * QB + np.arange(WINDOW + QB)[None, :]
    kb = kp[:, idx]
    vb = vp[:, idx]
    kpos = jnp.asarray(idx - WINDOW, jnp.int32)
    qpos = jnp.arange(T).reshape(nq, QB)
    dist = qpos[:, :, None] - kpos[:, None, :]
    valid = (kpos[:, None, :] >= 0) & (dist >= 0) & (dist < WINDOW)
    qb = q5.reshape(B, NSA_KV_HEADS, NSA_GROUP, nq, QB, NSA_HD)
    s = jnp.einsum('bkgnqd,bnjkd->bkgnqj', qb, kb).astype(jnp.float32) * (NSA_HD ** -0.5) \
        - slopes[:, :, None, None, None] * dist.astype(jnp.float32)
    p = masked_softmax(s, valid)
    o = jnp.einsum('bkgnqj,bnjkd->bkgnqd', p.astype(vb.dtype), vb)
    return o.reshape(B, NSA_KV_HEADS, NSA_GROUP, T, NSA_HD)


def window_sample(q5, kw, vw, qpos, kpos, slopes):
    dist = qpos[:, None] - kpos[None, :]
    valid = (dist >= 0) & (dist < WINDOW)
    s = jnp.einsum('bkgqd,bjkd->bkgqj', q5, kw).astype(jnp.float32) * (NSA_HD ** -0.5) \
        - slopes[:, :, None, None] * dist.astype(jnp.float32)
    p = masked_softmax(s, valid)
    return jnp.einsum('bkgqj,bjkd->bkgqd', p.astype(vw.dtype), vw)


def trunk_layer(x, p, kv_past, win_buf, ret_state, prompt, slopes, lw):
    B, T = x.shape[:2]
    h = rms_norm(x, lw['g_attn'])
    proj = h @ lw['w_in']
    rq, rk, rv, rg, nq, nkv, ngate, gate_a, gate_b = jnp.split(proj, list(np.cumsum(IN_SPLITS)[:-1]), axis=-1)
    o_r, ret_new = retention(rq.reshape(B, T, RET_HEADS, RET_DK), rk.reshape(B, T, RET_HEADS, RET_DK),
                             rv.reshape(B, T, RET_HEADS, RET_DV), ret_state)
    o_r = rms_norm(o_r.astype(x.dtype), lw['g_ret']).reshape(B, T, RET_VW)
    o_r = jax.nn.silu(rg) * o_r
    q = rms_norm(nq.reshape(B, T, NSA_HEADS, NSA_HD), lw['g_q'])
    q5 = q.reshape(B, T, NSA_KV_HEADS, NSA_GROUP, NSA_HD).transpose(0, 2, 3, 1, 4)
    nkv = nkv.reshape(B, T, 6, NSA_KV_HEADS, NSA_HD)
    k_sel = rms_norm(nkv[:, :, 2], lw['g_k_sel'])
    k_win = rms_norm(nkv[:, :, 4], lw['g_k_win'])
    kv_new = jnp.stack([nkv[:, :, 0], nkv[:, :, 1], k_sel, nkv[:, :, 3]], axis=2)
    win_rows = jnp.stack([k_win, nkv[:, :, 5]], axis=2)
    if prompt:
        qpos = jnp.arange(T)
        kv_all = kv_new
        o_win = window_prompt(q5, k_win, nkv[:, :, 5], slopes)
        win_state = win_rows[:, T - min(WINDOW, T):]
    else:
        P = kv_past.shape[1]
        qpos = P + jnp.arange(T)
        kv_all = jnp.concatenate([kv_past.astype(kv_new.dtype), kv_new], axis=1)
        wb = win_buf.shape[1]
        win_all = jnp.concatenate([win_buf.astype(win_rows.dtype), win_rows], axis=1)
        kpos = P - wb + jnp.arange(wb + T)
        o_win = window_sample(q5, win_all[:, :, 0], win_all[:, :, 1], qpos, kpos, slopes)
        win_state = win_all[:, T:]
    o_cmp, o_sel = nsa_compressed_selected(q5, kv_all, qpos, slopes, lw)
    g = jax.nn.sigmoid(ngate.astype(jnp.float32)).reshape(B, T, NSA_KV_HEADS, NSA_GROUP, 3).transpose(0, 2, 3, 1, 4)
    o_nsa = g[..., 0:1] * o_cmp + g[..., 1:2] * o_sel + g[..., 2:3] * o_win
    o_nsa = o_nsa.transpose(0, 3, 1, 2, 4).reshape(B, T, NSA_QW).astype(x.dtype)
    z = jax.nn.sigmoid(gate_a) * (o_r @ lw['w_ret_up']) + jax.nn.sigmoid(gate_b) * (o_nsa @ lw['w_nsa_up'])
    x = x + z @ lw['w_out']
    h2 = rms_norm(x, lw['g_mlp'])
    x = x + jnp.square(jax.nn.relu(h2 @ lw['w_up'])) @ lw['w_down']
    e = rms_norm(p.astype(x.dtype) @ lw['w_ple'], lw['g_ple'])
    x = x + jax.nn.sigmoid(rms_norm(x, lw['g_ple_gate']) @ lw['w_ple_gate']) * e
    return x, kv_new, win_state, ret_new


def setup_inputs(seed: int = 0) -> dict:
    key = jax.random.key(seed)
    ks = iter(jax.random.split(key, 48))

    def nrm(shape, scale):
        return jax.random.normal(next(ks), shape, jnp.float32) * scale

    def gain(shape):
        return 1.0 + nrm(shape, 0.05)

    n_pages = PAST_LEN // PAGE_SIZE
    used = DEC_BATCH * n_pages
    n_phys = used + max(1, used // 4)
    wb = min(WINDOW, PAST_LEN)
    L = DEPTH
    x_prompt = nrm((BATCH, SEQ, D_MODEL), 1.0)
    x_sample = nrm((DEC_BATCH, DEC_SEQ, D_MODEL), 1.0)
    p_prompt = nrm((L, BATCH, SEQ, PLE_DIM), 1.0)
    p_sample = nrm((L, DEC_BATCH, DEC_SEQ, PLE_DIM), 1.0)
    cache_kv = nrm((L, n_phys, PAGE_SIZE, 4, NSA_KV_HEADS, NSA_HD), 1.0)
    cache_win = nrm((L, DEC_BATCH, wb, 2, NSA_KV_HEADS, NSA_HD), 1.0)
    state_ret = nrm((L, DEC_BATCH, RET_HEADS, RET_DK, RET_DV), 0.1)
    page_table = jax.random.permutation(next(ks), n_phys)[:used].reshape(DEC_BATCH, n_pages).astype(jnp.int32)
    return {
        "x_prompt": x_prompt, "x_sample": x_sample, "p_prompt": p_prompt, "p_sample": p_sample,
        "cache_kv": cache_kv, "cache_win": cache_win, "state_ret": state_ret, "page_table": page_table,
        "w_in": nrm((L, D_MODEL, IN_COLS), D_MODEL ** -0.5),
        "g_attn": gain((L, D_MODEL)),
        "g_ret": gain((L, RET_HEADS, RET_DV)),
        "g_q": gain((L, NSA_HD)),
        "g_k_cmp": gain((L, NSA_HD)),
        "g_k_sel": gain((L, NSA_HD)),
        "g_k_win": gain((L, NSA_HD)),
        "pos_ck": nrm((L, CMP_LEN, NSA_HD), 0.5),
        "w_ck1": nrm((L, CMP_LEN, NSA_HD, NSA_HD), (CMP_LEN * NSA_HD) ** -0.5),
        "w_ck2": nrm((L, NSA_HD, NSA_HD), NSA_HD ** -0.5),
        "pos_cv": nrm((L, CMP_LEN, NSA_HD), 0.5),
        "w_cv1": nrm((L, CMP_LEN, NSA_HD, NSA_HD), (CMP_LEN * NSA_HD) ** -0.5),
        "w_cv2": nrm((L, NSA_HD, NSA_HD), NSA_HD ** -0.5),
        "w_ret_up": nrm((L, RET_VW, D_MODEL), RET_VW ** -0.5),
        "w_nsa_up": nrm((L, NSA_QW, D_MODEL), NSA_QW ** -0.5),
        "w_out": nrm((L, D_MODEL, D_MODEL), D_MODEL ** -0.5),
        "g_mlp": gain((L, D_MODEL)),
        "w_up": nrm((L, D_MODEL, D_FF), D_MODEL ** -0.5),
        "w_down": nrm((L, D_FF, D_MODEL), D_FF ** -0.5),
        "w_ple": nrm((L, PLE_DIM, D_MODEL), PLE_DIM ** -0.5),
        "g_ple": gain((L, D_MODEL)),
        "g_ple_gate": gain((L, D_MODEL)),
        "w_ple_gate": nrm((L, D_MODEL, D_MODEL), D_MODEL ** -0.5),
    }


def reference(x_prompt, x_sample, p_prompt, p_sample, cache_kv, cache_win, state_ret, page_table,
              w_in, g_attn, g_ret, g_q, g_k_cmp, g_k_sel, g_k_win,
              pos_ck, w_ck1, w_ck2, pos_cv, w_cv1, w_cv2,
              w_ret_up, w_nsa_up, w_out, g_mlp, w_up, w_down,
              w_ple, g_ple, g_ple_gate, w_ple_gate):
    slopes = alibi_slopes()
    yp, ys = x_prompt, x_sample
    kvp_l, kvs_l, wp_l, ws_l, rp_l, rs_l = [], [], [], [], [], []
    for i in range(DEPTH):
        lw = dict(w_in=w_in[i], g_attn=g_attn[i], g_ret=g_ret[i], g_q=g_q[i], g_k_cmp=g_k_cmp[i],
                  g_k_sel=g_k_sel[i], g_k_win=g_k_win[i], pos_ck=pos_ck[i], w_ck1=w_ck1[i], w_ck2=w_ck2[i],
                  pos_cv=pos_cv[i], w_cv1=w_cv1[i], w_cv2=w_cv2[i], w_ret_up=w_ret_up[i],
                  w_nsa_up=w_nsa_up[i], w_out=w_out[i], g_mlp=g_mlp[i], w_up=w_up[i], w_down=w_down[i],
                  w_ple=w_ple[i], g_ple=g_ple[i], g_ple_gate=g_ple_gate[i], w_ple_gate=w_ple_gate[i])
        past = cache_kv[i][page_table]
        past = past.reshape(page_table.shape[0], page_table.shape[1] * cache_kv.shape[2], *cache_kv.shape[3:])
        zero_state = jnp.zeros((yp.shape[0], RET_HEADS, RET_DK, RET_DV), jnp.float32)
        yp, kvp, wp, rp = trunk_layer(yp, p_prompt[i], None, None, zero_state, True, slopes, lw)
        ys, kvs, ws, rs = trunk_layer(ys, p_sample[i], past, cache_win[i], state_ret[i], False, slopes, lw)
        kvp_l.append(kvp)
        kvs_l.append(kvs)
        wp_l.append(wp)
        ws_l.append(ws)
        rp_l.append(rp)
        rs_l.append(rs)
    return (yp, ys, jnp.stack(kvp_l), jnp.stack(kvs_l), jnp.stack(wp_l), jnp.stack(ws_l), jnp.stack(rp_l), jnp.stack(rs_l))
```

```python
import functools
import math

import jax
import jax.numpy as jnp
import numpy as np
from jax import lax
from jax.experimental import pallas as pl
from jax.experimental.pallas import tpu as pltpu

F32 = jnp.float32
BF16 = jnp.bfloat16

D_MODEL = 4096
RET_HEADS = 8
RET_DK = 256
RET_DV = 256
RET_CHUNK = 128
NSA_HEADS = 16
NSA_KV_HEADS = 4
NSA_GROUP = 4
NSA_HD = 128
CMP_LEN = 32
CMP_STRIDE = 16
SEL_LEN = 64
SEL_TOPN = 16
WINDOW = 512
PAGE = 128
EPS = 1e-6
NEG = -1e30
FORCE = 1e6
SCALE = NSA_HD ** -0.5

RET_W = RET_HEADS * RET_DK
NSA_QW = NSA_HEADS * NSA_HD
NSA_KVW = NSA_KV_HEADS * NSA_HD
COL_RET = 0
COL_NQ = 4 * RET_W
COL_NKV = COL_NQ + NSA_QW
COL_NGATE = COL_NKV + 6 * NSA_KVW
COL_GATES = COL_NGATE + NSA_HEADS * 3
LANES = 128
VMEM_LIMIT = 56 * 1024 * 1024


def _cparams(sem, vmem=VMEM_LIMIT):
    return pltpu.CompilerParams(dimension_semantics=sem, vmem_limit_bytes=vmem)


def _sigmoid(x):
    return 1.0 / (1.0 + jnp.exp(-x))


def _dot(a, b):
    return jnp.dot(a, b, preferred_element_type=F32)


def _dot_nt(a, b):
    return lax.dot_general(a, b, (((1,), (1,)), ((), ())), preferred_element_type=F32)


def _masked_softmax(s, valid):
    sm = jnp.where(valid, s, NEG)
    m = jnp.max(sm, axis=-1, keepdims=True)
    e = jnp.where(valid, jnp.exp(sm - m), 0.0)
    l = jnp.sum(e, axis=-1, keepdims=True)
    return e / jnp.where(l > 0.0, l, 1.0)


def _rmsnorm_body(x_ref, g_ref, o_ref):
    x = x_ref[...]
    ms = jnp.mean(x * x, axis=-1, keepdims=True)
    o_ref[...] = ((x * lax.rsqrt(ms + EPS)) * g_ref[...]).astype(o_ref.dtype)


def rmsnorm(x, g, out_dtype=BF16, tm=256):
    M, D = x.shape
    tm = min(tm, M)
    return pl.pallas_call(
        _rmsnorm_body,
        grid=(M // tm,),
        in_specs=[pl.BlockSpec((tm, D), lambda i: (i, 0)), pl.BlockSpec((1, D), lambda i: (0, 0))],
        out_specs=pl.BlockSpec((tm, D), lambda i: (i, 0)),
        out_shape=jax.ShapeDtypeStruct((M, D), out_dtype),
        compiler_params=_cparams(("parallel",)),
        name="rmsnorm",
    )(x, g.reshape(1, D))


def _mm_body(*refs, n_extra, nk, epilogue):
    a_ref, b_ref = refs[0], refs[1]
    extra = refs[2:2 + n_extra]
    o_ref = refs[2 + n_extra]
    a = a_ref[...]
    b = b_ref[...]
    if a.dtype != BF16:
        a = a.astype(BF16)
    if b.dtype != BF16:
        b = b.astype(BF16)
    part = _dot(a, b)
    if nk == 1:
        o_ref[...] = epilogue(part, *[r[...] for r in extra]).astype(o_ref.dtype)
    else:
        acc_ref = refs[3 + n_extra]
        k = pl.program_id(2)

        @pl.when(k == 0)
        def _():
            acc_ref[...] = part

        @pl.when(k > 0)
        def _():
            acc_ref[...] += part

        @pl.when(k == nk - 1)
        def _():
            o_ref[...] = epilogue(acc_ref[...], *[r[...] for r in extra]).astype(o_ref.dtype)


def matmul(a, b, *, out_dtype, tm, tn, tk=None, col0=0, ncols=None, extras=(), epilogue=None, name="matmul"):
    M, K = a.shape
    N = ncols if ncols is not None else b.shape[1]
    tk = tk or K
    tm = min(tm, M)
    tn = min(tn, N)
    assert M % tm == 0 and N % tn == 0 and K % tk == 0 and col0 % tn == 0
    nk = K // tk
    cb0 = col0 // tn
    if epilogue is None:
        epilogue = lambda acc: acc
    in_specs = [pl.BlockSpec((tm, tk), lambda j, i, k: (i, k)),
                pl.BlockSpec((tk, tn), lambda j, i, k: (k, j + cb0))]
    ops = [a, b]
    for arr, kind in extras:
        if kind == "mn":
            in_specs.append(pl.BlockSpec((tm, tn), lambda j, i, k: (i, j)))
        else:
            in_specs.append(pl.BlockSpec((1, tn), lambda j, i, k: (0, j)))
        ops.append(arr)
    return pl.pallas_call(
        functools.partial(_mm_body, n_extra=len(extras), nk=nk, epilogue=epilogue),
        grid=(N // tn, M // tm, nk),
        in_specs=in_specs,
        out_specs=pl.BlockSpec((tm, tn), lambda j, i, k: (i, j)),
        out_shape=jax.ShapeDtypeStruct((M, N), out_dtype),
        scratch_shapes=[pltpu.VMEM((tm, tn), F32)] if nk > 1 else [],
        compiler_params=_cparams(("parallel", "parallel", "arbitrary")),
        name=name,
    )(*ops)


def _group_norm_epilogue(acc, g, flag):
    tm, tn = acc.shape
    pieces = []
    for c in range(tn // LANES):
        x = acc[:, c * LANES:(c + 1) * LANES]
        ms = jnp.mean(x * x, axis=-1, keepdims=True)
        gc = g[:, c * LANES:(c + 1) * LANES]
        fc = flag[:, c * LANES:(c + 1) * LANES]
        y = (x * lax.rsqrt(ms + EPS)) * gc
        pieces.append(jnp.where(fc > 0.5, y, x))
    return jnp.concatenate(pieces, axis=-1)


def _ret_consts(C):
    f32 = F32
    log_g = jnp.log(1.0 - 2.0 ** (-5.0 - jnp.arange(RET_HEADS, dtype=f32)))
    i = jnp.arange(C, dtype=f32)
    diff = i[:, None] - i[None, :]
    dm = jnp.where(diff >= 0, jnp.exp(log_g[:, None, None] * jnp.maximum(diff, 0.0)), 0.0)
    qd = jnp.exp(log_g[:, None] * (i + 1.0))[:, :, None]
    kd = jnp.exp(log_g[:, None] * (C - 1.0 - i))[:, :, None]
    cd = jnp.exp(log_g * C)[:, None, None] * jnp.ones((RET_HEADS, 1, LANES), f32)
    return dm, qd, kd, cd


def _ret_chunk(q, k, v, S, dm, qd, kd, cd):
    qb = q.astype(BF16)
    kk = k * (RET_DK ** -0.5)
    vb = v.astype(BF16)
    inner = _dot_nt(qb, kk.astype(BF16)) * dm
    o = _dot(inner.astype(BF16), vb) + _dot(qb, S.astype(BF16)) * qd
    kkd_t = (kk * kd).T.astype(BF16)
    S_new = S * cd + _dot(kkd_t, vb)
    return o, S_new


def _ret_out(o, g, gret):
    ms = jnp.mean(o * o, axis=-1, keepdims=True)
    on = (o * lax.rsqrt(ms + EPS)) * gret
    return (g * _sigmoid(g)) * on


def _ret_prompt_body(q_ref, k_ref, v_ref, g_ref, gret_ref, dm_ref, qd_ref, kd_ref, cd_ref, o_ref, s_ref, st_ref):
    c = pl.program_id(2)

    @pl.when(c == 0)
    def _():
        st_ref[...] = jnp.zeros_like(st_ref)

    o, S_new = _ret_chunk(q_ref[0], k_ref[0], v_ref[0], st_ref[...], dm_ref[0], qd_ref[0], kd_ref[0],
                          cd_ref[0][:, 0:1])
    st_ref[...] = S_new
    o_ref[0] = _ret_out(o, g_ref[0], gret_ref[0]).astype(o_ref.dtype)
    s_ref[0, 0] = S_new


def retention_prompt(R, g_ret, B, T):
    C = math.gcd(T, RET_CHUNK)
    n = T // C
    dm, qd, kd, cd = _ret_consts(C)
    R3 = R.reshape(B, T, 4 * RET_W)
    H = RET_HEADS

    def col(off):
        return pl.BlockSpec((1, C, RET_DK), lambda b, h, c: (b, c, off + h))

    o, s = pl.pallas_call(
        _ret_prompt_body,
        grid=(B, H, n),
        in_specs=[col(0), col(H), col(2 * H), col(3 * H),
                  pl.BlockSpec((1, 1, RET_DV), lambda b, h, c: (h, 0, 0)),
                  pl.BlockSpec((1, C, C), lambda b, h, c: (h, 0, 0)),
                  pl.BlockSpec((1, C, 1), lambda b, h, c: (h, 0, 0)),
                  pl.BlockSpec((1, C, 1), lambda b, h, c: (h, 0, 0)),
                  pl.BlockSpec((1, 1, LANES), lambda b, h, c: (h, 0, 0))],
        out_specs=[pl.BlockSpec((1, C, RET_DV), lambda b, h, c: (b, c, h)),
                   pl.BlockSpec((1, 1, RET_DK, RET_DV), lambda b, h, c: (b, h, 0, 0))],
        out_shape=[jax.ShapeDtypeStruct((B, T, RET_W), BF16),
                   jax.ShapeDtypeStruct((B, H, RET_DK, RET_DV), F32)],
        scratch_shapes=[pltpu.VMEM((RET_DK, RET_DV), F32)],
        compiler_params=_cparams(("parallel", "parallel", "arbitrary")),
        name="retention_prompt",
    )(R3, R3, R3, R3, g_ret.reshape(H, 1, RET_DV), dm, qd, kd, cd)
    return o.reshape(B * T, RET_W), s


def _ret_sample_body(q_ref, k_ref, v_ref, g_ref, s0_ref, gret_ref, dm_ref, qd_ref, kd_ref, cd_ref, o_ref, s_ref):
    outs = []
    for h in range(RET_HEADS):
        sl = slice(h * RET_DK, (h + 1) * RET_DK)
        o, S_new = _ret_chunk(q_ref[0][:, sl], k_ref[0][:, sl], v_ref[0][:, sl], s0_ref[0, h],
                              dm_ref[h], qd_ref[h], kd_ref[h], cd_ref[h][:, 0:1])
        s_ref[0, h] = S_new
        outs.append(_ret_out(o, g_ref[0][:, sl], gret_ref[h]))
    o_ref[0] = jnp.concatenate(outs, axis=-1).astype(o_ref.dtype)


def retention_sample(R, g_ret, state, DB, TS):
    C = math.gcd(TS, RET_CHUNK)
    assert C == TS
    dm, qd, kd, cd = _ret_consts(C)
    R3 = R.reshape(DB, TS, 4 * RET_W)
    H = RET_HEADS

    def col(j):
        return pl.BlockSpec((1, TS, RET_W), lambda b: (b, 0, j))

    def full(shape):
        return pl.BlockSpec(shape, lambda b: (0,) * len(shape))

    o, s = pl.pallas_call(
        _ret_sample_body,
        grid=(DB,),
        in_specs=[col(0), col(1), col(2), col(3),
                  pl.BlockSpec((1, H, RET_DK, RET_DV), lambda b: (b, 0, 0, 0)),
                  full((H, 1, RET_DV)), full((H, C, C)), full((H, C, 1)), full((H, C, 1)), full((H, 1, LANES))],
        out_specs=[pl.BlockSpec((1, TS, RET_W), lambda b: (b, 0, 0)),
                   pl.BlockSpec((1, H, RET_DK, RET_DV), lambda b: (b, 0, 0, 0))],
        out_shape=[jax.ShapeDtypeStruct((DB, TS, RET_W), BF16),
                   jax.ShapeDtypeStruct((DB, H, RET_DK, RET_DV), F32)],
        compiler_params=_cparams(("parallel",)),
        name="retention_sample",
    )(R3, R3, R3, R3, state, g_ret.reshape(H, 1, RET_DV), dm, qd, kd, cd)
    return o.reshape(DB * TS, RET_W), s


def _cmp_bias_body(post_ref, w1_ref, o_ref):
    acc = jnp.zeros((NSA_HD, NSA_HD), F32)
    for p in range(CMP_LEN):
        acc = acc + post_ref[:, p:p + 1] * w1_ref[p]
    o_ref[...] = jnp.sum(acc, axis=0, keepdims=True)


def compress_bias(pos, w1):
    return pl.pallas_call(
        _cmp_bias_body,
        out_shape=jax.ShapeDtypeStruct((1, NSA_HD), F32),
        name="compress_bias",
    )(pos.T, w1)


def _pair_weights(w1):
    w = w1.reshape(2, CMP_STRIDE // 2, 2, NSA_HD, NSA_HD)
    w = w.transpose(1, 2, 3, 0, 4)
    return w.reshape(CMP_STRIDE // 2, 2 * NSA_HD, 2 * NSA_HD).astype(BF16)


def _gelu_tanh(x):
    return 0.5 * x * (1.0 + jnp.tanh(math.sqrt(2.0 / math.pi) * (x + 0.044715 * (x * x * x))))


def _compress_body(*refs, n_pages, prefetch):
    if prefetch:
        refs = refs[1:]
    pages = refs[:n_pages]
    wk_ref, wv_ref, bk_ref, bv_ref, w2k_ref, w2v_ref, gk_ref = refs[n_pages:n_pages + 7]
    ck_ref, cv_ref = refs[n_pages + 7:]
    segs_per_page = PAGE // CMP_STRIDE
    nseg = n_pages * segs_per_page
    rows = NSA_KV_HEADS * nseg
    acc = [jnp.zeros((rows, 2 * NSA_HD), F32), jnp.zeros((rows, 2 * NSA_HD), F32)]
    ncg = 2 * NSA_KV_HEADS

    def seg_rows(p, cg):
        return jnp.concatenate(
            [pg[0, pl.ds(p, segs_per_page, stride=CMP_STRIDE), cg, :] for pg in pages], axis=0)

    for pp in range(CMP_STRIDE // 2):
        for slot, w_ref in enumerate((wk_ref, wv_ref)):
            lhs = jnp.concatenate(
                [jnp.concatenate([seg_rows(2 * pp + i, slot * NSA_KV_HEADS + k) for i in range(2)], axis=-1)
                 for k in range(NSA_KV_HEADS)], axis=0)
            acc[slot] = acc[slot] + _dot(lhs.astype(BF16), w_ref[pp])
    for slot, (b_ref, w2_ref, o_ref) in enumerate(((bk_ref, w2k_ref, ck_ref), (bv_ref, w2v_ref, cv_ref))):
        p0 = acc[slot][:, :NSA_HD]
        p1 = acc[slot][:, NSA_HD:]
        a = p0 + pltpu.roll(p1, rows - 1, 0) + b_ref[...]
        y = _dot(_gelu_tanh(a).astype(BF16), w2_ref[...].astype(BF16))
        if slot == 0:
            ms = jnp.mean(y * y, axis=-1, keepdims=True)
            y = (y * lax.rsqrt(ms + EPS)) * gk_ref[...]
        o_ref[0] = y.astype(o_ref.dtype)


def compress(kv, page_table, n_pages, lw, bias_k, bias_v):
    nseg = n_pages * (PAGE // CMP_STRIDE)
    rows = NSA_KV_HEADS * nseg
    wk = _pair_weights(lw["w_ck1"])
    wv = _pair_weights(lw["w_cv1"])
    prefetch = page_table is not None
    ncg = 2 * NSA_KV_HEADS
    kv = kv.reshape(kv.shape[0], kv.shape[1], 2, ncg, NSA_HD)
    page_block = (1, PAGE, None, ncg, NSA_HD)
    if prefetch:
        NBATCH = page_table.shape[0]
        page_specs = [pl.BlockSpec(page_block, functools.partial(lambda b, pt, j: (pt[b, j], 0, 0, 0, 0), j=j))
                      for j in range(n_pages)]
        cmap = lambda b, pt: (0, 0)
        cmap3 = lambda b, pt: (0, 0, 0)
        omap = lambda b, pt: (b, 0, 0)
    else:
        NBATCH = kv.shape[0]
        page_specs = [pl.BlockSpec(page_block, functools.partial(lambda b, j: (b, j, 0, 0, 0), j=j))
                      for j in range(n_pages)]
        cmap = lambda b: (0, 0)
        cmap3 = lambda b: (0, 0, 0)
        omap = lambda b: (b, 0, 0)
    in_specs = page_specs + [
        pl.BlockSpec((CMP_STRIDE // 2, 2 * NSA_HD, 2 * NSA_HD), cmap3),
        pl.BlockSpec((CMP_STRIDE // 2, 2 * NSA_HD, 2 * NSA_HD), cmap3),
        pl.BlockSpec((1, NSA_HD), cmap), pl.BlockSpec((1, NSA_HD), cmap),
        pl.BlockSpec((NSA_HD, NSA_HD), cmap), pl.BlockSpec((NSA_HD, NSA_HD), cmap),
        pl.BlockSpec((1, NSA_HD), cmap)]
    out_specs = [pl.BlockSpec((1, rows, NSA_HD), omap), pl.BlockSpec((1, rows, NSA_HD), omap)]
    grid_spec = pltpu.PrefetchScalarGridSpec(
        num_scalar_prefetch=1 if prefetch else 0, grid=(NBATCH,), in_specs=in_specs, out_specs=out_specs)
    args = ([page_table] if prefetch else []) + [kv] * n_pages + [
        wk, wv, bias_k, bias_v, lw["w_ck2"], lw["w_cv2"], lw["g_k_cmp"].reshape(1, NSA_HD)]
    return pl.pallas_call(
        functools.partial(_compress_body, n_pages=n_pages, prefetch=prefetch),
        grid_spec=grid_spec,
        out_shape=[jax.ShapeDtypeStruct((NBATCH, rows, NSA_HD), BF16)] * 2,
        compiler_params=_cparams(("parallel",)),
        name="nsa_compress",
    )(*args)


def _overlap_matrix():
    c = np.arange(LANES)[:, None] * CMP_STRIDE
    j = np.arange(LANES)[None, :] * SEL_LEN
    return jnp.asarray((c < j + SEL_LEN) & (c + CMP_LEN > j), BF16)


def _expand_matrix(n_keys):
    j = np.arange(LANES)[:, None]
    t = np.arange(n_keys)[None, :] // SEL_LEN
    return jnp.asarray(j == t, BF16)


def _topk_mask(imp, qpos, nb):
    R = imp.shape[0]
    jb = lax.broadcasted_iota(jnp.int32, (R, LANES), 1)
    cur = qpos // SEL_LEN
    visible = jb * SEL_LEN <= qpos
    forced = (jb == 0) | (jb == cur) | (jb == cur - 1)
    score = jnp.where(visible, imp + jnp.where(forced, FORCE, 0.0), -FORCE * FORCE)
    dead = -3.0e38
    score = jnp.where(jb < nb, score, dead)
    jf = jb.astype(F32)
    sel = jnp.zeros((R, LANES), F32)
    for _ in range(min(SEL_TOPN, nb)):
        m = jnp.max(score, axis=-1, keepdims=True)
        idx = jnp.min(jnp.where(score == m, jf, float(LANES)), axis=-1, keepdims=True)
        pick = jf == idx
        sel = jnp.where(pick, 1.0, sel)
        score = jnp.where(pick, dead, score)
    return sel


def _nsa_prompt_body(slopes_ref, q_ref, ck_ref, cv_ref, ks_ref, vs_ref, kw_ref, vw_ref, gt_ref, ov_ref, ex_ref,
                     o_ref, *, tq, T, nb):
    kvh = pl.program_id(1)
    qi = pl.program_id(2)
    q0 = qi * tq
    G = NSA_GROUP
    R = G * tq
    qt = q_ref[0]
    q4 = jnp.concatenate([qt[:, g * NSA_HD:(g + 1) * NSA_HD] for g in range(G)], axis=0).astype(BF16)
    trow = lax.broadcasted_iota(jnp.int32, (tq, 1), 0) + q0
    qpos4 = jnp.concatenate([trow] * G, axis=0)
    slope4 = jnp.concatenate([jnp.full((tq, 1), slopes_ref[kvh * G + g], F32) for g in range(G)], axis=0)

    ckb = ck_ref[0]
    cvb = cv_ref[0]
    nc_total = ckb.shape[0]
    cidx = lax.broadcasted_iota(jnp.int32, (R, nc_total), 1)
    end_pos = cidx * CMP_STRIDE + (CMP_LEN - 1)
    dist = (qpos4 - end_pos).astype(F32)
    s = _dot_nt(q4, ckb) * SCALE - slope4 * dist
    valid = (end_pos <= qpos4) & (cidx < nc_total - 1)
    p_c = _masked_softmax(s, valid)
    p_cb = p_c.astype(BF16)
    o_cmp = _dot(p_cb, cvb)
    imp4 = _dot(p_cb, ov_ref[...])
    imp = imp4[0:tq]
    for g in range(1, G):
        imp = imp + imp4[g * tq:(g + 1) * tq]
    sel = _topk_mask(imp, trow, nb)

    maskf = _dot(sel.astype(BF16), ex_ref[...])
    mask4 = jnp.concatenate([maskf] * G, axis=0)
    kpos = lax.broadcasted_iota(jnp.int32, (R, T), 1)
    dist = (qpos4 - kpos).astype(F32)
    s = _dot_nt(q4, ks_ref[0].astype(BF16)) * SCALE - slope4 * dist
    valid = (mask4 > 0.5) & (kpos <= qpos4)
    p = _masked_softmax(s, valid)
    o_sel = _dot(p.astype(BF16), vs_ref[0].astype(BF16))

    WK = WINDOW + tq
    start = jnp.maximum(q0 - WINDOW, 0)
    start = pl.multiple_of(start, tq)
    kw = kw_ref[0, pl.ds(start, WK), :].astype(BF16)
    vw = vw_ref[0, pl.ds(start, WK), :].astype(BF16)
    kpos = lax.broadcasted_iota(jnp.int32, (R, WK), 1) + start
    disti = qpos4 - kpos
    s = _dot_nt(q4, kw) * SCALE - slope4 * disti.astype(F32)
    valid = (disti >= 0) & (disti < WINDOW)
    p = _masked_softmax(s, valid)
    o_win = _dot(p.astype(BF16), vw)

    gates = _sigmoid(gt_ref[0])
    outs = []
    for g in range(G):
        rs = slice(g * tq, (g + 1) * tq)
        outs.append(gates[:, 3 * g:3 * g + 1] * o_cmp[rs] + gates[:, 3 * g + 1:3 * g + 2] * o_sel[rs]
                    + gates[:, 3 * g + 2:3 * g + 3] * o_win[rs])
    o_ref[0] = jnp.concatenate(outs, axis=-1).astype(o_ref.dtype)


def nsa_prompt(slopes, Q, ck, cv, KV, WR, GT, B, T, tq=128):
    assert T % tq == 0 and T >= WINDOW + tq
    nb = -(-T // SEL_LEN)
    nseg = T // CMP_STRIDE
    KVH = NSA_KV_HEADS
    smem = pl.BlockSpec(memory_space=pltpu.SMEM)
    return pl.pallas_call(
        functools.partial(_nsa_prompt_body, tq=tq, T=T, nb=nb),
        grid=(B, KVH, T // tq),
        in_specs=[smem,
                  pl.BlockSpec((1, tq, NSA_KVW), lambda b, k, i: (b, i, k)),
                  pl.BlockSpec((1, nseg, NSA_HD), lambda b, k, i: (b, k, 0)),
                  pl.BlockSpec((1, nseg, NSA_HD), lambda b, k, i: (b, k, 0)),
                  pl.BlockSpec((1, T, NSA_HD), lambda b, k, i: (b, 0, 2 * KVH + k)),
                  pl.BlockSpec((1, T, NSA_HD), lambda b, k, i: (b, 0, 3 * KVH + k)),
                  pl.BlockSpec((1, T, NSA_HD), lambda b, k, i: (b, 0, k)),
                  pl.BlockSpec((1, T, NSA_HD), lambda b, k, i: (b, 0, KVH + k)),
                  pl.BlockSpec((1, tq, LANES), lambda b, k, i: (b, i, k)),
                  pl.BlockSpec((LANES, LANES), lambda b, k, i: (0, 0)),
                  pl.BlockSpec((LANES, T), lambda b, k, i: (0, 0))],
        out_specs=pl.BlockSpec((1, tq, NSA_KVW), lambda b, k, i: (b, i, k)),
        out_shape=jax.ShapeDtypeStruct((B, T, NSA_QW), BF16),
        compiler_params=_cparams(("parallel", "parallel", "arbitrary")),
        name="nsa_prompt",
    )(slopes, Q, ck, cv, KV, KV, WR, WR, GT, _overlap_matrix(), _expand_matrix(T))


def _rows_by_head(x, width):
    return jnp.concatenate([x[:, h * width:(h + 1) * width] for h in range(NSA_HEADS)], axis=0)


def _diag_blocks(x, rows_per_kvh):
    return jnp.concatenate([x[k * rows_per_kvh:(k + 1) * rows_per_kvh, k * NSA_HD:(k + 1) * NSA_HD]
                            for k in range(NSA_KV_HEADS)], axis=0)


def _nsa_sample_body(*refs, n_pages, TS, P):
    pt_ref = refs[0]
    del pt_ref
    (slope_ref, q_ref, ck_ref, cv_ref) = refs[1:5]
    pages = refs[5:5 + n_pages]
    kvn_ref, cw_ref, wr_ref, gt_ref, ov_ref, ex_ref = refs[5 + n_pages:11 + n_pages]
    o_ref, wout_ref = refs[11 + n_pages:]
    G, KVH, HD = NSA_GROUP, NSA_KV_HEADS, NSA_HD
    R = NSA_HEADS * TS
    rk = G * TS
    nb = -(-(P + TS) // SEL_LEN)
    qf = _rows_by_head(q_ref[0], HD)
    qb = qf.astype(BF16)
    zeros_blk = jnp.zeros((rk, HD), F32)
    qbd = jnp.concatenate(
        [jnp.concatenate([qf[k * rk:(k + 1) * rk] if kk == k else zeros_blk for kk in range(KVH)], axis=0)
         for k in range(KVH)], axis=-1).astype(BF16)
    slope = slope_ref[...]
    row = lax.broadcasted_iota(jnp.int32, (R, 1), 0)
    t_row = row % TS
    qpos = P + t_row

    ckb = ck_ref[0]
    cvb = cv_ref[0]
    nseg = ckb.shape[0] // KVH
    cidx = lax.broadcasted_iota(jnp.int32, (R, nseg), 1)
    end_pos = cidx * CMP_STRIDE + (CMP_LEN - 1)
    s = jnp.concatenate([_dot_nt(qb[k * rk:(k + 1) * rk], ckb[k * nseg:(k + 1) * nseg]) for k in range(KVH)], axis=0)
    s = s * SCALE - slope * (qpos - end_pos).astype(F32)
    valid = (end_pos <= qpos) & (cidx < nseg - 1)
    p_cb = _masked_softmax(s, valid).astype(BF16)
    o_cmp = jnp.concatenate([_dot(p_cb[k * rk:(k + 1) * rk], cvb[k * nseg:(k + 1) * nseg]) for k in range(KVH)], axis=0)
    imp_r = _dot(p_cb, ov_ref[...])
    imps = []
    for k in range(KVH):
        a = imp_r[k * rk:k * rk + TS]
        for g in range(1, G):
            a = a + imp_r[k * rk + g * TS:k * rk + (g + 1) * TS]
        imps.append(a)
    imp = jnp.concatenate(imps, axis=0)
    qpos_kt = P + lax.broadcasted_iota(jnp.int32, (KVH * TS, 1), 0) % TS
    sel = _topk_mask(imp, qpos_kt, nb)
    sel_r = jnp.concatenate([sel[k * TS:(k + 1) * TS] for k in range(KVH) for _ in range(G)], axis=0)
    sel_rb = sel_r.astype(BF16)

    kvn = kvn_ref[0]
    pad = jnp.zeros((PAGE - TS, KVH * HD), F32)
    k_new = jnp.concatenate([kvn[:, 2 * KVH * HD:3 * KVH * HD], pad], axis=0).astype(BF16)
    v_new = jnp.concatenate([kvn[:, 3 * KVH * HD:4 * KVH * HD], pad], axis=0).astype(BF16)
    s_tiles = [_dot_nt(qbd, pg[0][:, :KVH * HD].astype(BF16)) for pg in pages] + [_dot_nt(qbd, k_new)]
    s = jnp.concatenate(s_tiles, axis=-1)
    nk = P + PAGE
    kpos = lax.broadcasted_iota(jnp.int32, (R, nk), 1)
    maskf = _dot(sel_rb, ex_ref[...])
    s = s * SCALE - slope * (qpos - kpos).astype(F32)
    valid = (maskf > 0.5) & (kpos <= qpos) & (kpos < P + TS)
    pb = _masked_softmax(s, valid).astype(BF16)
    acc = _dot(pb[:, P:], v_new)
    for j, pg in enumerate(pages):
        acc = acc + _dot(pb[:, j * PAGE:(j + 1) * PAGE], pg[0][:, KVH * HD:].astype(BF16))
    o_sel = _diag_blocks(acc, rk)

    WB = cw_ref.shape[1]
    wr = wr_ref[0]
    kw_new = jnp.concatenate([wr[:, :KVH * HD], pad], axis=0).astype(BF16)
    vw_new = jnp.concatenate([wr[:, KVH * HD:], pad], axis=0).astype(BF16)
    cw = cw_ref[0]
    s = jnp.concatenate([_dot_nt(qbd, cw[:, :KVH * HD].astype(BF16)), _dot_nt(qbd, kw_new)], axis=-1)
    widx = lax.broadcasted_iota(jnp.int32, (R, WB + PAGE), 1)
    disti = (WB + t_row) - widx
    s = s * SCALE - slope * disti.astype(F32)
    valid = (disti >= 0) & (disti < WINDOW) & (widx < WB + TS)
    pb = _masked_softmax(s, valid).astype(BF16)
    acc = _dot(pb[:, :WB], cw[:, KVH * HD:].astype(BF16)) + _dot(pb[:, WB:], vw_new)
    o_win = _diag_blocks(acc, rk)

    gates = _sigmoid(gt_ref[0])
    outs = []
    for h in range(NSA_HEADS):
        k, g = divmod(h, G)
        c0 = k * LANES + 3 * g
        rs = slice(h * TS, (h + 1) * TS)
        outs.append(gates[:, c0:c0 + 1] * o_cmp[rs] + gates[:, c0 + 1:c0 + 2] * o_sel[rs]
                    + gates[:, c0 + 2:c0 + 3] * o_win[rs])
    o_ref[0] = jnp.concatenate(outs, axis=-1).astype(o_ref.dtype)
    wout_ref[0, 0:WB - TS, :] = cw_ref[0, TS:WB, :]
    wout_ref[0, WB - TS:WB, :] = wr


def nsa_sample(slopes, page_table, Q, ck, cv, cache, KVN, cache_win, WR, GT, DB, TS):
    n_pages = page_table.shape[1]
    P = n_pages * PAGE
    WB = cache_win.shape[1]
    assert WB == WINDOW and TS % 8 == 0 and TS <= SEL_LEN
    rows = ck.shape[1]
    R = NSA_HEADS * TS
    slope_rows = jnp.repeat(slopes, TS).reshape(R, 1)
    page_specs = [pl.BlockSpec((1, PAGE, 2 * NSA_KVW), functools.partial(lambda b, pt, j: (pt[b, j], 0, 1), j=j))
                  for j in range(n_pages)]
    in_specs = [pl.BlockSpec((R, 1), lambda b, pt: (0, 0)),
                pl.BlockSpec((1, TS, NSA_QW), lambda b, pt: (b, 0, 0)),
                pl.BlockSpec((1, rows, NSA_HD), lambda b, pt: (b, 0, 0)),
                pl.BlockSpec((1, rows, NSA_HD), lambda b, pt: (b, 0, 0))] + page_specs + [
                pl.BlockSpec((1, TS, 4 * NSA_KVW), lambda b, pt: (b, 0, 0)),
                pl.BlockSpec((1, WB, 2 * NSA_KVW), lambda b, pt: (b, 0, 0)),
                pl.BlockSpec((1, TS, 2 * NSA_KVW), lambda b, pt: (b, 0, 0)),
                pl.BlockSpec((1, TS, NSA_KV_HEADS * LANES), lambda b, pt: (b, 0, 0)),
                pl.BlockSpec((LANES, LANES), lambda b, pt: (0, 0)),
                pl.BlockSpec((LANES, P + PAGE), lambda b, pt: (0, 0))]
    out_specs = [pl.BlockSpec((1, TS, NSA_QW), lambda b, pt: (b, 0, 0)),
                 pl.BlockSpec((1, WB, 2 * NSA_KVW), lambda b, pt: (b, 0, 0))]
    grid_spec = pltpu.PrefetchScalarGridSpec(num_scalar_prefetch=1, grid=(DB,), in_specs=in_specs, out_specs=out_specs)
    return pl.pallas_call(
        functools.partial(_nsa_sample_body, n_pages=n_pages, TS=TS, P=P),
        grid_spec=grid_spec,
        out_shape=[jax.ShapeDtypeStruct((DB, TS, NSA_QW), BF16),
                   jax.ShapeDtypeStruct((DB, WB, 2 * NSA_KVW), F32)],
        compiler_params=_cparams(("parallel",)),
        name="nsa_sample",
    )(page_table, slope_rows, Q, ck, cv, *([cache] * n_pages), KVN, cache_win, WR, GT,
      _overlap_matrix(), _expand_matrix(P + PAGE))


def _merge_body(or_ref, on_ref, wr_ref, wn_ref, ga_ref, gb_ref, z_ref):
    a = _dot(or_ref[...], wr_ref[...])
    b = _dot(on_ref[...], wn_ref[...])
    z_ref[...] = (_sigmoid(ga_ref[...]) * a + _sigmoid(gb_ref[...]) * b).astype(z_ref.dtype)


def merge(o_r, o_n, w_r, w_n, gates, tm=512, tn=1024):
    M, K = o_r.shape
    N = w_r.shape[1]
    tm = min(tm, M)
    nb = N // tn
    return pl.pallas_call(
        _merge_body,
        grid=(nb, M // tm),
        in_specs=[pl.BlockSpec((tm, K), lambda j, i: (i, 0)), pl.BlockSpec((tm, K), lambda j, i: (i, 0)),
                  pl.BlockSpec((K, tn), lambda j, i: (0, j)), pl.BlockSpec((K, tn), lambda j, i: (0, j)),
                  pl.BlockSpec((tm, tn), lambda j, i: (i, j)), pl.BlockSpec((tm, tn), lambda j, i: (i, j + nb))],
        out_specs=pl.BlockSpec((tm, tn), lambda j, i: (i, j)),
        out_shape=jax.ShapeDtypeStruct((M, N), BF16),
        compiler_params=_cparams(("parallel", "parallel")),
        name="merge",
    )(o_r, o_n, w_r, w_n, gates, gates)


def _ple_body(p_ref, w_ref, g_ref, o_ref):
    e = _dot(p_ref[...].astype(BF16), w_ref[...].astype(BF16))
    ms = jnp.mean(e * e, axis=-1, keepdims=True)
    o_ref[...] = (e * lax.rsqrt(ms + EPS)) * g_ref[...]


def ple_embed(p, w, g, tm=256):
    M, K = p.shape
    N = w.shape[1]
    tm = min(tm, M)
    return pl.pallas_call(
        _ple_body,
        grid=(M // tm,),
        in_specs=[pl.BlockSpec((tm, K), lambda i: (i, 0)), pl.BlockSpec((K, N), lambda i: (0, 0)),
                  pl.BlockSpec((1, N), lambda i: (0, 0))],
        out_specs=pl.BlockSpec((tm, N), lambda i: (i, 0)),
        out_shape=jax.ShapeDtypeStruct((M, N), F32),
        compiler_params=_cparams(("parallel",)),
        name="ple_embed",
    )(p, w, g.reshape(1, N))


def _prep_weights(lw):
    w_in = lw["w_in"]
    wp = {}
    wp["w_main"] = w_in[:, :COL_NGATE].astype(BF16)
    wng = w_in[:, COL_NGATE:COL_GATES].reshape(D_MODEL, NSA_KV_HEADS, NSA_GROUP * 3)
    wng = jnp.pad(wng, ((0, 0), (0, 0), (0, LANES - NSA_GROUP * 3)))
    wp["w_ngate"] = wng.reshape(D_MODEL, NSA_KV_HEADS * LANES).astype(BF16)
    wp["w_gates"] = w_in[:, COL_GATES:].astype(BF16)
    for name in ("w_ret_up", "w_nsa_up", "w_out", "w_up", "w_down", "w_ple_gate"):
        wp[name] = lw[name].astype(BF16)
    ones = jnp.ones((NSA_KVW,), F32)
    zeros = jnp.zeros((NSA_KVW,), F32)
    wp["g_nq"] = jnp.tile(lw["g_q"], NSA_HEADS).reshape(1, NSA_QW)
    wp["f_nq"] = jnp.ones((1, NSA_QW), F32)
    gk = jnp.tile(lw["g_k_sel"], NSA_KV_HEADS)
    gw = jnp.tile(lw["g_k_win"], NSA_KV_HEADS)
    wp["g_nkv"] = jnp.concatenate([ones, ones, gk, ones, gw, ones]).reshape(1, 6 * NSA_KVW)
    wp["f_nkv"] = jnp.concatenate([zeros, zeros, ones, zeros, ones, zeros]).reshape(1, 6 * NSA_KVW)
    return wp


def _in_proj(x2, lw, wp):
    h = rmsnorm(x2, lw["g_attn"])
    tm = 512
    R = matmul(h, wp["w_main"], out_dtype=F32, tm=tm, tn=1024, col0=COL_RET, ncols=4 * RET_W, name="proj_ret")
    Q = matmul(h, wp["w_main"], out_dtype=F32, tm=tm, tn=1024, col0=COL_NQ, ncols=NSA_QW,
               extras=[(wp["g_nq"], "n"), (wp["f_nq"], "n")], epilogue=_group_norm_epilogue, name="proj_nq")
    KVW = matmul(h, wp["w_main"], out_dtype=F32, tm=tm, tn=1024, col0=COL_NKV, ncols=6 * NSA_KVW,
                 extras=[(wp["g_nkv"], "n"), (wp["f_nkv"], "n")], epilogue=_group_norm_epilogue, name="proj_nkv")
    GT = matmul(h, wp["w_ngate"], out_dtype=F32, tm=tm, tn=512, name="proj_ngate")
    GAB = matmul(h, wp["w_gates"], out_dtype=F32, tm=tm, tn=1024, name="proj_gates")
    return R, Q, KVW, GT, GAB


def _token_chain(x2, o_r, o_n, GAB, p2, lw, wp):
    z = merge(o_r, o_n, wp["w_ret_up"], wp["w_nsa_up"], GAB)
    x1 = matmul(z, wp["w_out"], out_dtype=F32, tm=512, tn=1024, extras=[(x2, "mn")],
                epilogue=lambda acc, r: r + acc, name="out_proj")
    h2 = rmsnorm(x1, lw["g_mlp"])
    u = matmul(h2, wp["w_up"], out_dtype=BF16, tm=512, tn=1024,
               epilogue=lambda acc: jnp.square(jnp.maximum(acc, 0.0)), name="mlp_up")
    x2b = matmul(u, wp["w_down"], out_dtype=F32, tm=512, tn=1024, tk=4096, extras=[(x1, "mn")],
                 epilogue=lambda acc, r: r + acc, name="mlp_down")
    e = ple_embed(p2, lw["w_ple"], lw["g_ple"])
    hg = rmsnorm(x2b, lw["g_ple_gate"])
    y = matmul(hg, wp["w_ple_gate"], out_dtype=F32, tm=512, tn=1024, extras=[(x2b, "mn"), (e, "mn")],
               epilogue=lambda acc, r, ee: r + _sigmoid(acc) * ee, name="ple_gate")
    return y


def _alibi_slopes():
    m = 2.0 ** (-8.0 * np.arange(1, NSA_HEADS + 1) / NSA_HEADS)
    return jnp.asarray(m, F32)


def _layer(x_prompt, x_sample, p_prompt, p_sample, cache_kv, cache_win, state_ret, page_table, lw):
    B, T, D = x_prompt.shape
    DB, TS, _ = x_sample.shape
    n_pages = page_table.shape[1]
    slopes = _alibi_slopes()
    wp = _prep_weights(lw)
    bias_k = compress_bias(lw["pos_ck"], lw["w_ck1"])
    bias_v = compress_bias(lw["pos_cv"], lw["w_cv1"])

    xp = x_prompt.reshape(B * T, D)
    R, Q, KVW, GT, GAB = _in_proj(xp, lw, wp)
    o_r, ret_p = retention_prompt(R, lw["g_ret"], B, T)
    KV = KVW[:, :4 * NSA_KVW]
    WR = KVW[:, 4 * NSA_KVW:]
    KV3 = KV.reshape(B, T, 4 * NSA_KVW)
    WR3 = WR.reshape(B, T, 2 * NSA_KVW)
    ck, cv = compress(KV3, None, T // PAGE, lw, bias_k, bias_v)
    o_n = nsa_prompt(slopes, Q.reshape(B, T, NSA_QW), ck, cv, KV3, WR3, GT.reshape(B, T, -1), B, T)
    yp = _token_chain(xp, o_r, o_n.reshape(B * T, NSA_QW), GAB, p_prompt.reshape(B * T, -1), lw, wp)
    kv_p = KV3.reshape(B, T, 4, NSA_KV_HEADS, NSA_HD)
    wb_p = min(WINDOW, T)
    win_p = WR3[:, T - wb_p:].reshape(B, wb_p, 2, NSA_KV_HEADS, NSA_HD)

    xs = x_sample.reshape(DB * TS, D)
    R, Q, KVW, GT, GAB = _in_proj(xs, lw, wp)
    o_r, ret_s = retention_sample(R, lw["g_ret"], state_ret, DB, TS)
    KVs = KVW[:, :4 * NSA_KVW].reshape(DB, TS, 4 * NSA_KVW)
    WRs = KVW[:, 4 * NSA_KVW:].reshape(DB, TS, 2 * NSA_KVW)
    cache2 = cache_kv.reshape(cache_kv.shape[0], PAGE, 4 * NSA_KVW)
    ck, cv = compress(cache2, page_table, n_pages, lw, bias_k, bias_v)
    WB = cache_win.shape[1]
    o_n, win_s = nsa_sample(slopes, page_table, Q.reshape(DB, TS, NSA_QW), ck, cv, cache2, KVs,
                            cache_win.reshape(DB, WB, 2 * NSA_KVW), WRs, GT.reshape(DB, TS, -1), DB, TS)
    ys = _token_chain(xs, o_r, o_n.reshape(DB * TS, NSA_QW), GAB, p_sample.reshape(DB * TS, -1), lw, wp)
    kv_s = KVs.reshape(DB, TS, 4, NSA_KV_HEADS, NSA_HD)
    win_s = win_s.reshape(DB, WB, 2, NSA_KV_HEADS, NSA_HD)
    return (yp.reshape(B, T, D), ys.reshape(DB, TS, D), kv_p, kv_s, win_p, win_s, ret_p, ret_s)


_WEIGHT_NAMES = ("w_in", "g_attn", "g_ret", "g_q", "g_k_cmp", "g_k_sel", "g_k_win", "pos_ck", "w_ck1", "w_ck2",
                 "pos_cv", "w_cv1", "w_cv2", "w_ret_up", "w_nsa_up", "w_out", "g_mlp", "w_up", "w_down",
                 "w_ple", "g_ple", "g_ple_gate", "w_ple_gate")


def kernel(x_prompt, x_sample, p_prompt, p_sample, cache_kv, cache_win, state_ret, page_table, w_in, g_attn, g_ret, g_q, g_k_cmp, g_k_sel, g_k_win, pos_ck, w_ck1, w_ck2, pos_cv, w_cv1, w_cv2, w_ret_up, w_nsa_up, w_out, g_mlp, w_up, w_down, w_ple, g_ple, g_ple_gate, w_ple_gate):
    weights = dict(zip(_WEIGHT_NAMES, (w_in, g_attn, g_ret, g_q, g_k_cmp, g_k_sel, g_k_win, pos_ck, w_ck1, w_ck2,
                                       pos_cv, w_cv1, w_cv2, w_ret_up, w_nsa_up, w_out, g_mlp, w_up, w_down,
                                       w_ple, g_ple, g_ple_gate, w_ple_gate)))
    depth = w_in.shape[0]
    yp, ys = x_prompt, x_sample
    outs = [[] for _ in range(6)]
    for i in range(depth):
        lw = {k: v[i] for k, v in weights.items()}
        res = _layer(yp, ys, p_prompt[i], p_sample[i], cache_kv[i], cache_win[i], state_ret[i], page_table, lw)
        yp, ys = res[0], res[1]
        for lst, r in zip(outs, res[2:]):
            lst.append(r)
    return (yp, ys) + tuple(jnp.stack(l) for l in outs)
```
